```python
import math
import jax, jax.numpy as jnp
from jax import lax
import numpy as np

D_MODEL = 4096
BATCH = 1
SEQ = 8192
DEPTH = 2

HEAD_DIM = 128
N_HEADS_A = 16
N_KV_HEADS_A = 4
ATTN_WIDTH = N_HEADS_A * HEAD_DIM
KV_WIDTH = N_KV_HEADS_A * HEAD_DIM
IDX_HEADS = 16
IDX_DIM = 64
TOPK_MAX = 256
Q_BLOCK = 128
ROPE_THETA = 500000.0
ROPE_FRAC_DEN = 4
SGU_CHUNK = 128
SGU_GROUPS = 8
SGU_GROUP_DIM = 128
SGU_WIDTH = SGU_GROUPS * SGU_GROUP_DIM
POOL_WINDOWS = (2, 4, 8, 16)
POOL_GROUPS = 4
POOL_GROUP_DIM = 256
POOL_WIDTH = POOL_GROUPS * POOL_GROUP_DIM
D_FF = 11008
N_BRANCHES = 3
EPS = 1e-6
SPLITS = (ATTN_WIDTH, KV_WIDTH, KV_WIDTH, IDX_HEADS * IDX_DIM, IDX_DIM, IDX_HEADS,
          2 * SGU_WIDTH, POOL_WIDTH, N_BRANCHES * D_MODEL)
N_IN = sum(SPLITS)

kernel_name = "hybrid_dsa_sgu_pool_gated_macaron"


def rms_norm(x, g):
    xf = x.astype(jnp.float32)
    y = xf * lax.rsqrt(jnp.mean(xf * xf, axis=-1, keepdims=True) + EPS)
    return (y * g.astype(jnp.float32)).astype(x.dtype)


def swiglu(x, w_in, w_out):
    g, u = jnp.split(x @ w_in, 2, axis=-1)
    return (jax.nn.silu(g) * u) @ w_out


def partial_rope(x, positions):
    d = x.shape[-1]
    rd = d // ROPE_FRAC_DEN
    half = rd // 2
    inv_freq = ROPE_THETA ** (-(jnp.arange(0, rd, 2, dtype=jnp.float32) / rd))
    ang = positions.astype(jnp.float32)[..., None] * inv_freq
    cos = jnp.cos(ang)[:, :, None, :]
    sin = jnp.sin(ang)[:, :, None, :]
    xr = x[..., :rd].astype(jnp.float32)
    x1, x2 = xr[..., :half], xr[..., half:]
    rot = jnp.concatenate([x1 * cos - x2 * sin, x2 * cos + x1 * sin], axis=-1).astype(x.dtype)
    return jnp.concatenate([rot, x[..., rd:]], axis=-1)


def dsa_attention(q, k, v, iq, ik, iw):
    B, S = q.shape[0], q.shape[1]
    topk = min(TOPK_MAX, S // 4)
    nb = S // Q_BLOCK
    G = N_HEADS_A // N_KV_HEADS_A
    idx_scale = (IDX_HEADS ** -0.5) * (IDX_DIM ** -0.5)
    ikf = ik.astype(jnp.float32)
    key_pos = jnp.arange(S, dtype=jnp.int32)

    def to_blocks(a):
        return jnp.moveaxis(a.reshape((B, nb, Q_BLOCK) + a.shape[2:]), 1, 0)

    def one_block(args):
        qb, iqb, iwb, start = args
        qpos = start + jnp.arange(Q_BLOCK, dtype=jnp.int32)
        dots = jnp.einsum('bthd,bsd->bths', iqb.astype(jnp.float32), ikf)
        score = jnp.einsum('bth,bths->bts', iwb.astype(jnp.float32), jax.nn.relu(dots)) * idx_scale
        causal = key_pos[None, :] <= qpos[:, None]
        score = jnp.where(causal[None], score, -jnp.inf)
        _, idx = lax.top_k(score, topk)
        kg = jax.vmap(lambda kb, ib: kb[ib])(k, idx)
        vg = jax.vmap(lambda vb, ib: vb[ib])(v, idx)
        qg = qb.reshape(B, Q_BLOCK, N_KV_HEADS_A, G, HEAD_DIM)
        logits = jnp.einsum('btjgd,btnjd->btjgn', qg, kg).astype(jnp.float32) * (HEAD_DIM ** -0.5)
        valid = (idx <= qpos[None, :, None])[:, :, None, None, :]
        logits = jnp.where(valid, logits, -jnp.inf)
        p = jax.nn.softmax(logits, axis=-1).astype(v.dtype)
        o = jnp.einsum('btjgn,btnjd->btjgd', p, vg)
        return o.reshape(B, Q_BLOCK, ATTN_WIDTH)

    starts = jnp.arange(nb, dtype=jnp.int32) * Q_BLOCK
    out = lax.map(one_block, (to_blocks(q), to_blocks(iq), to_blocks(iw), starts))
    return jnp.moveaxis(out, 0, 1).reshape(B, S, ATTN_WIDTH)


def spatial_gating(z, ln_g, ln_b, w_s, b_s):
    B, S = z.shape[0], z.shape[1]
    z = jax.nn.gelu(z)
    u, v = jnp.split(z, 2, axis=-1)
    vf = v.astype(jnp.float32)
    mu = jnp.mean(vf, axis=-1, keepdims=True)
    var = jnp.mean(jnp.square(vf - mu), axis=-1, keepdims=True)
    v = ((vf - mu) * lax.rsqrt(var + EPS) * ln_g.astype(jnp.float32) + ln_b.astype(jnp.float32)).astype(z.dtype)
    nc = S // SGU_CHUNK
    v = v.reshape(B, nc, SGU_CHUNK, SGU_GROUPS, SGU_GROUP_DIM)
    mask = jnp.tril(jnp.ones((SGU_CHUNK, SGU_CHUNK), dtype=bool))
    w = jnp.where(mask, w_s, 0)
    s = jnp.einsum('gij,bcjgd->bcigd', w, v) + b_s.T[None, None, :, :, None]
    return u * s.reshape(B, S, SGU_WIDTH)


def multiscale_pool(c, w_pool, scale):
    B, S = c.shape[0], c.shape[1]
    cg = c.reshape(B, S, POOL_GROUPS, POOL_GROUP_DIM)
    cs = jnp.cumsum(cg.astype(jnp.float32), axis=1)
    cs = jnp.concatenate([jnp.zeros_like(cs[:, :1]), cs], axis=1)
    t = jnp.arange(S, dtype=jnp.int32)[:, None]
    win = jnp.array(POOL_WINDOWS, dtype=jnp.int32)[None, :]
    lo = jnp.maximum(t + 1 - win, 0)
    lower = cs[:, lo, jnp.arange(POOL_GROUPS)[None, :], :]
    count = (t + 1 - lo).astype(jnp.float32)
    mean = (cs[:, 1:] - lower) / count[None, :, :, None]
    y = (mean - cg.astype(jnp.float32)).astype(c.dtype)
    y = jnp.einsum('bsgc,gcd->bsgd', y, w_pool).reshape(B, S, POOL_WIDTH)
    return y * scale


def setup_inputs(seed: int = 0) -> dict:
    key = jax.random.key(seed)
    ks = jax.random.split(key, 24)
    L = DEPTH

    def normal(k, shape, s):
        return jax.random.normal(k, shape, jnp.float32) * s

    def gain(k, shape, s=0.02):
        return 1.0 + s * jax.random.normal(k, shape, jnp.float32)

    return {
        "x": normal(ks[0], (BATCH, SEQ, D_MODEL), 1.0),
        "positions": jnp.broadcast_to(jnp.arange(SEQ, dtype=jnp.int32)[None, :], (BATCH, SEQ)),
        "ffn1_norm": gain(ks[1], (L, D_MODEL)),
        "ffn1_w_in": normal(ks[2], (L, D_MODEL, 2 * D_FF), D_MODEL ** -0.5),
        "ffn1_w_out": normal(ks[3], (L, D_FF, D_MODEL), D_FF ** -0.5),
        "mix_norm": gain(ks[4], (L, D_MODEL)),
        "w_in": normal(ks[5], (L, D_MODEL, N_IN), D_MODEL ** -0.5),
        "sgu_ln_g": gain(ks[6], (L, SGU_WIDTH)),
        "sgu_ln_b": normal(ks[7], (L, SGU_WIDTH), 0.02),
        "sgu_w": normal(ks[8], (L, SGU_GROUPS, SGU_CHUNK, SGU_CHUNK), 0.5 * SGU_CHUNK ** -0.5),
        "sgu_b": gain(ks[9], (L, SGU_GROUPS, SGU_CHUNK), 0.1),
        "pool_w": normal(ks[10], (L, POOL_GROUPS, POOL_GROUP_DIM, POOL_GROUP_DIM), POOL_GROUP_DIM ** -0.5),
        "pool_scale": gain(ks[11], (L, POOL_WIDTH), 0.1),
        "proj_a": normal(ks[12], (L, ATTN_WIDTH, D_MODEL), ATTN_WIDTH ** -0.5),
        "proj_b": normal(ks[13], (L, SGU_WIDTH, D_MODEL), SGU_WIDTH ** -0.5),
        "proj_c": normal(ks[14], (L, POOL_WIDTH, D_MODEL), POOL_WIDTH ** -0.5),
        "w_out": normal(ks[15], (L, D_MODEL, D_MODEL), D_MODEL ** -0.5),
        "ffn2_norm": gain(ks[16], (L, D_MODEL)),
        "ffn2_w_in": normal(ks[17], (L, D_MODEL, 2 * D_FF), D_MODEL ** -0.5),
        "ffn2_w_out": normal(ks[18], (L, D_FF, D_MODEL), D_FF ** -0.5),
        "final_norm": gain(ks[19], (D_MODEL,)),
    }


def reference(x, positions, ffn1_norm, ffn1_w_in, ffn1_w_out, mix_norm, w_in, sgu_ln_g, sgu_ln_b,
              sgu_w, sgu_b, pool_w, pool_scale, proj_a, proj_b, proj_c, w_out, ffn2_norm,
              ffn2_w_in, ffn2_w_out, final_norm):
    B, S = x.shape[0], x.shape[1]
    cuts = np.cumsum(SPLITS)[:-1].tolist()
    h = x
    for l in range(DEPTH):
        h = h + 0.5 * swiglu(rms_norm(h, ffn1_norm[l]), ffn1_w_in[l], ffn1_w_out[l])
        n = rms_norm(h, mix_norm[l])
        q, k, v, iq, ik, iw, z, c, gates = jnp.split(n @ w_in[l], cuts, axis=-1)
        q = partial_rope(q.reshape(B, S, N_HEADS_A, HEAD_DIM), positions)
        k = partial_rope(k.reshape(B, S, N_KV_HEADS_A, HEAD_DIM), positions)
        v = v.reshape(B, S, N_KV_HEADS_A, HEAD_DIM)
        iq = partial_rope(iq.reshape(B, S, IDX_HEADS, IDX_DIM), positions)
        ik = partial_rope(ik.reshape(B, S, 1, IDX_DIM), positions)[:, :, 0]
        y_a = dsa_attention(q, k, v, iq, ik, iw)
        y_b = spatial_gating(z, sgu_ln_g[l], sgu_ln_b[l], sgu_w[l], sgu_b[l])
        y_c = multiscale_pool(c, pool_w[l], pool_scale[l])
        g_a, g_b, g_c = jnp.split(jax.nn.sigmoid(gates), N_BRANCHES, axis=-1)
        merged = g_a * (y_a @ proj_a[l]) + g_b * (y_b @ proj_b[l]) + g_c * (y_c @ proj_c[l])
        h = h + merged @ w_out[l]
        h = h + 0.5 * swiglu(rms_norm(h, ffn2_norm[l]), ffn2_w_in[l], ffn2_w_out[l])
    return rms_norm(h, final_norm)
```

```python
import functools
import math

import jax
import jax.numpy as jnp
from jax import lax
from jax.experimental import pallas as pl
from jax.experimental.pallas import tpu as pltpu

F32 = jnp.float32
BF16 = jnp.bfloat16
I32 = jnp.int32

HEAD_DIM = 128
N_HEADS_A = 16
N_KV_HEADS_A = 4
GQA = N_HEADS_A // N_KV_HEADS_A
ATTN_WIDTH = N_HEADS_A * HEAD_DIM
KV_WIDTH = N_KV_HEADS_A * HEAD_DIM
IDX_HEADS = 16
IDX_DIM = 64
IDX_WIDTH = IDX_HEADS * IDX_DIM
TOPK_MAX = 256
ROPE_THETA = 500000.0
ROPE_FRAC_DEN = 4
SGU_CHUNK = 128
SGU_GROUPS = 8
SGU_GROUP_DIM = 128
SGU_WIDTH = SGU_GROUPS * SGU_GROUP_DIM
POOL_WINDOWS = (2, 4, 8, 16)
POOL_GROUPS = 4
POOL_GROUP_DIM = 256
POOL_WIDTH = POOL_GROUPS * POOL_GROUP_DIM
POOL_HALO = 16
N_BRANCHES = 3
EPS = 1e-6

LANES = 128
VMEM_LIMIT = 56 * 1024 * 1024
INT_MIN = -(2 ** 31)
NEG_BIG = -1e30

OFF_QKVI = 0
W_QKVI = ATTN_WIDTH + 2 * KV_WIDTH + IDX_WIDTH
OFF_IKIW = W_QKVI
W_IKIW = IDX_DIM + IDX_HEADS
OFF_Z = OFF_IKIW + W_IKIW
OFF_C = OFF_Z + 2 * SGU_WIDTH
OFF_G = OFF_C + POOL_WIDTH


def _pick(n, prefs):
    for p in prefs:
        if n % p == 0:
            return p
    return n


def _cparams(sem):
    return pltpu.CompilerParams(dimension_semantics=sem, vmem_limit_bytes=VMEM_LIMIT)


def _resident(shape, index_map):
    return pl.BlockSpec(shape, index_map, pipeline_mode=pl.Buffered(1))


def _rmsnorm_kernel(h_ref, g_ref, o_ref):
    x = h_ref[...]
    ms = jnp.mean(x * x, axis=-1, keepdims=True)
    y = x * lax.rsqrt(ms + EPS)
    o_ref[...] = (y * g_ref[...]).astype(o_ref.dtype)


def rmsnorm(h, gain, out_dtype):
    S, D = h.shape
    tm = _pick(S, (256, 128, 8))
    return pl.pallas_call(
        _rmsnorm_kernel,
        grid=(S // tm,),
        in_specs=[pl.BlockSpec((tm, D), lambda i: (i, 0)), pl.BlockSpec((1, D), lambda i: (0, 0))],
        out_specs=pl.BlockSpec((tm, D), lambda i: (i, 0)),
        out_shape=jax.ShapeDtypeStruct((S, D), out_dtype),
        compiler_params=_cparams(("parallel",)),
        name="rmsnorm",
    )(h, gain.reshape(1, D))


def _ffn_up_kernel(n_ref, wg_ref, wu_ref, o_ref):
    n = n_ref[...]
    g = jnp.dot(n, wg_ref[...], preferred_element_type=F32)
    u = jnp.dot(n, wu_ref[...], preferred_element_type=F32)
    o_ref[...] = (g * jax.nn.sigmoid(g) * u).astype(o_ref.dtype)


def ffn_up(n, w_in):
    S, D = n.shape
    F = w_in.shape[1] // 2
    tm = _pick(S, (1024, 512, 256, 128))
    tn = _pick(F, (256, 128))
    nj = F // tn
    return pl.pallas_call(
        _ffn_up_kernel,
        grid=(S // tm, nj),
        in_specs=[
            pl.BlockSpec((tm, D), lambda i, j: (i, 0)),
            pl.BlockSpec((D, tn), lambda i, j: (0, j)),
            pl.BlockSpec((D, tn), lambda i, j: (0, j + nj)),
        ],
        out_specs=pl.BlockSpec((tm, tn), lambda i, j: (i, j)),
        out_shape=jax.ShapeDtypeStruct((S, F), BF16),
        compiler_params=_cparams(("parallel", "arbitrary")),
        name="ffn_up",
    )(n, w_in, w_in)


def _mm_res_kernel(a_ref, w_ref, h_ref, o_ref, *, scale):
    acc = jnp.dot(a_ref[...], w_ref[...], preferred_element_type=F32)
    o_ref[...] = h_ref[...] + scale * acc


def matmul_residual(a, w, h, scale):
    S, K = a.shape
    N = w.shape[1]
    tm = _pick(S, (512, 256, 128))
    tn = _pick(N, (512, 256, 128))
    return pl.pallas_call(
        functools.partial(_mm_res_kernel, scale=scale),
        grid=(S // tm, N // tn),
        in_specs=[
            pl.BlockSpec((tm, K), lambda i, j: (i, 0)),
            pl.BlockSpec((K, tn), lambda i, j: (0, j)),
            pl.BlockSpec((tm, tn), lambda i, j: (i, j)),
        ],
        out_specs=pl.BlockSpec((tm, tn), lambda i, j: (i, j)),
        out_shape=jax.ShapeDtypeStruct((S, N), F32),
        compiler_params=_cparams(("parallel", "arbitrary")),
        name="matmul_residual",
    )(a, w, h)


def _rope_tables(positions, head_dim, period, active_lanes):
    rd = head_dim // ROPE_FRAC_DEN
    half = rd // 2
    inv_freq = ROPE_THETA ** (-(jnp.arange(0, rd, 2, dtype=F32) / rd))
    ang = positions.astype(F32)[:, None] * inv_freq[None, :]
    cos, sin = jnp.cos(ang), jnp.sin(ang)
    lane = jnp.arange(LANES)
    p = lane % period
    act = lane < active_lanes
    idx = p % half
    cos_l, sin_l = cos[:, idx], sin[:, idx]
    first = act & (p < half)
    second = act & (p >= half) & (p < rd)
    c = jnp.where(first | second, cos_l, 1.0)
    s1 = jnp.where(first, -sin_l, 0.0)
    s2 = jnp.where(second, sin_l, 0.0)
    return c.astype(F32), s1.astype(F32), s2.astype(F32), half


def _rope_apply(x, c, s1, s2, half):
    outs = []
    for b in range(x.shape[1] // LANES):
        xb = x[:, b * LANES:(b + 1) * LANES]
        outs.append(xb * c + pltpu.roll(xb, LANES - half, 1) * s1 + pltpu.roll(xb, half, 1) * s2)
    return outs[0] if len(outs) == 1 else jnp.concatenate(outs, axis=1)


def _qkvi_kernel(n_ref, w_ref, c128_ref, s1_128_ref, s2_128_ref, c64_ref, s1_64_ref, s2_64_ref, o_ref,
                 *, tn, half128, half64):
    j = pl.program_id(1)
    acc = jnp.dot(n_ref[...], w_ref[...], preferred_element_type=F32)
    n_qk = (ATTN_WIDTH + KV_WIDTH) // tn
    n_v = KV_WIDTH // tn

    @pl.when(j < n_qk)
    def _():
        o_ref[...] = _rope_apply(acc, c128_ref[...], s1_128_ref[...], s2_128_ref[...], half128).astype(o_ref.dtype)

    @pl.when((j >= n_qk) & (j < n_qk + n_v))
    def _():
        o_ref[...] = acc.astype(o_ref.dtype)

    @pl.when(j >= n_qk + n_v)
    def _():
        o_ref[...] = _rope_apply(acc, c64_ref[...], s1_64_ref[...], s2_64_ref[...], half64).astype(o_ref.dtype)


def qkvi_proj(n, w, tabs128, tabs64):
    S, D = n.shape
    N = w.shape[1]
    tm = _pick(S, (1024, 512, 256, 128))
    tn = 512
    c128, s1_128, s2_128, half128 = tabs128
    c64, s1_64, s2_64, half64 = tabs64
    tab = pl.BlockSpec((tm, LANES), lambda i, j: (i, 0))
    return pl.pallas_call(
        functools.partial(_qkvi_kernel, tn=tn, half128=half128, half64=half64),
        grid=(S // tm, N // tn),
        in_specs=[pl.BlockSpec((tm, D), lambda i, j: (i, 0)), pl.BlockSpec((D, tn), lambda i, j: (0, j)),
                  tab, tab, tab, tab, tab, tab],
        out_specs=pl.BlockSpec((tm, tn), lambda i, j: (i, j)),
        out_shape=jax.ShapeDtypeStruct((S, N), BF16),
        compiler_params=_cparams(("parallel", "arbitrary")),
        name="qkvi_proj",
    )(n, w, c128, s1_128, s2_128, c64, s1_64, s2_64)


def _ikiw_kernel(n_ref, w_ref, c_ref, s1_ref, s2_ref, o_ref, *, half):
    acc = jnp.dot(n_ref[...], w_ref[...], preferred_element_type=F32)
    o_ref[...] = _rope_apply(acc, c_ref[...], s1_ref[...], s2_ref[...], half)


def ikiw_proj(n, w, tabs):
    S, D = n.shape
    tm = _pick(S, (1024, 512, 256, 128))
    c, s1, s2, half = tabs
    tab = pl.BlockSpec((tm, LANES), lambda i: (i, 0))
    return pl.pallas_call(
        functools.partial(_ikiw_kernel, half=half),
        grid=(S // tm,),
        in_specs=[pl.BlockSpec((tm, D), lambda i: (i, 0)), _resident((D, LANES), lambda i: (0, 0)), tab, tab, tab],
        out_specs=pl.BlockSpec((tm, LANES), lambda i: (i, 0)),
        out_shape=jax.ShapeDtypeStruct((S, LANES), F32),
        compiler_params=_cparams(("parallel",)),
        name="ikiw_proj",
    )(n, w, c, s1, s2)


def _sgu_kernel(n_ref, wz_ref, lng_ref, lnb_ref, ws_ref, bst_ref, o_ref, *, tm):
    z = jnp.dot(n_ref[...], wz_ref[...], preferred_element_type=F32)
    z = jax.nn.gelu(z)
    u = z[:, :SGU_WIDTH]
    v = z[:, SGU_WIDTH:]
    mu = jnp.mean(v, axis=-1, keepdims=True)
    var = jnp.mean(jnp.square(v - mu), axis=-1, keepdims=True)
    vn = ((v - mu) * lax.rsqrt(var + EPS) * lng_ref[...] + lnb_ref[...]).astype(BF16)
    row = lax.broadcasted_iota(I32, (SGU_CHUNK, SGU_CHUNK), 0)
    col = lax.broadcasted_iota(I32, (SGU_CHUNK, SGU_CHUNK), 1)
    causal = col <= row
    for g in range(SGU_GROUPS):
        w = jnp.where(causal, ws_ref[g], 0.0).astype(BF16)
        bias = bst_ref[:, g:g + 1]
        gs = slice(g * SGU_GROUP_DIM, (g + 1) * SGU_GROUP_DIM)
        for c in range(tm // SGU_CHUNK):
            rs = slice(c * SGU_CHUNK, (c + 1) * SGU_CHUNK)
            s = jnp.dot(w, vn[rs, gs], preferred_element_type=F32) + bias
            o_ref[rs, gs] = (u[rs, gs] * s).astype(o_ref.dtype)


def sgu_branch(n, wz, ln_g, ln_b, w_s, b_s):
    S, D = n.shape
    tm = _pick(S, (512, 256, 128))
    return pl.pallas_call(
        functools.partial(_sgu_kernel, tm=tm),
        grid=(S // tm,),
        in_specs=[
            pl.BlockSpec((tm, D), lambda i: (i, 0)),
            _resident((D, 2 * SGU_WIDTH), lambda i: (0, 0)),
            pl.BlockSpec((1, SGU_WIDTH), lambda i: (0, 0)),
            pl.BlockSpec((1, SGU_WIDTH), lambda i: (0, 0)),
            pl.BlockSpec((SGU_GROUPS, SGU_CHUNK, SGU_CHUNK), lambda i: (0, 0, 0)),
            pl.BlockSpec((SGU_CHUNK, SGU_GROUPS), lambda i: (0, 0)),
        ],
        out_specs=pl.BlockSpec((tm, SGU_WIDTH), lambda i: (i, 0)),
        out_shape=jax.ShapeDtypeStruct((S, SGU_WIDTH), BF16),
        compiler_params=_cparams(("parallel",)),
        name="sgu_branch",
    )(n, wz, ln_g.reshape(1, SGU_WIDTH), ln_b.reshape(1, SGU_WIDTH), w_s, b_s.T)


def _pool_kernel(n_ref, wc_ref, pw_ref, sc_ref, o_ref, ext_ref, *, tm):
    i = pl.program_id(0)

    @pl.when(i == 0)
    def _():
        ext_ref[0:POOL_HALO, :] = jnp.zeros((POOL_HALO, POOL_WIDTH), F32)

    c = jnp.dot(n_ref[...], wc_ref[...], preferred_element_type=F32)
    ext_ref[POOL_HALO:POOL_HALO + tm, :] = c
    t = i * tm + lax.broadcasted_iota(I32, (tm, 1), 0)
    for g, win in enumerate(POOL_WINDOWS):
        gs = slice(g * POOL_GROUP_DIM, (g + 1) * POOL_GROUP_DIM)
        cg = c[:, gs]
        s = cg
        for d in range(1, win):
            s = s + ext_ref[POOL_HALO - d:POOL_HALO - d + tm, gs]
        count = jnp.minimum(t + 1, win).astype(F32)
        y = (s / count - cg).astype(BF16)
        out = jnp.dot(y, pw_ref[g], preferred_element_type=F32) * sc_ref[:, gs]
        o_ref[:, gs] = out.astype(o_ref.dtype)
    ext_ref[0:POOL_HALO, :] = c[tm - POOL_HALO:, :]


def pool_branch(n, wc, pool_w, pool_scale):
    S, D = n.shape
    tm = _pick(S, (512, 256, 128))
    return pl.pallas_call(
        functools.partial(_pool_kernel, tm=tm),
        grid=(S // tm,),
        in_specs=[
            pl.BlockSpec((tm, D), lambda i: (i, 0)),
            _resident((D, POOL_WIDTH), lambda i: (0, 0)),
            pl.BlockSpec((POOL_GROUPS, POOL_GROUP_DIM, POOL_GROUP_DIM), lambda i: (0, 0, 0)),
            pl.BlockSpec((1, POOL_WIDTH), lambda i: (0, 0)),
        ],
        out_specs=pl.BlockSpec((tm, POOL_WIDTH), lambda i: (i, 0)),
        out_shape=jax.ShapeDtypeStruct((S, POOL_WIDTH), BF16),
        scratch_shapes=[pltpu.VMEM((POOL_HALO + tm, POOL_WIDTH), F32)],
        compiler_params=_cparams(("arbitrary",)),
        name="pool_branch",
    )(n, wc, pool_w, pool_scale.reshape(1, POOL_WIDTH))


def _merge_kernel(n_ref, ya_ref, yb_ref, yc_ref, wga_ref, wgb_ref, wgc_ref, pa_ref, pb_ref, pc_ref, o_ref):
    n = n_ref[...]

    def branch(wg_ref, y_ref, p_ref):
        gate = jax.nn.sigmoid(jnp.dot(n, wg_ref[...], preferred_element_type=F32))
        return gate * jnp.dot(y_ref[...], p_ref[...], preferred_element_type=F32)

    m = branch(wga_ref, ya_ref, pa_ref) + branch(wgb_ref, yb_ref, pb_ref) + branch(wgc_ref, yc_ref, pc_ref)
    o_ref[...] = m.astype(o_ref.dtype)


def gated_merge(n, ya, yb, yc, wgates, pa, pb, pc):
    S, D = n.shape
    tm = _pick(S, (512, 256, 128))
    tn = _pick(D, (256, 128))
    nj = D // tn
    row = lambda w: pl.BlockSpec((tm, w), lambda i, j: (i, 0))
    col = lambda k, off: pl.BlockSpec((k, tn), lambda i, j: (0, j + off))
    return pl.pallas_call(
        _merge_kernel,
        grid=(S // tm, nj),
        in_specs=[row(D), row(ATTN_WIDTH), row(SGU_WIDTH), row(POOL_WIDTH),
                  col(D, 0), col(D, nj), col(D, 2 * nj),
                  col(ATTN_WIDTH, 0), col(SGU_WIDTH, 0), col(POOL_WIDTH, 0)],
        out_specs=pl.BlockSpec((tm, tn), lambda i, j: (i, j)),
        out_shape=jax.ShapeDtypeStruct((S, D), BF16),
        compiler_params=_cparams(("parallel", "arbitrary")),
        name="gated_merge",
    )(n, ya, yb, yc, wgates, wgates, wgates, pa, pb, pc)


def _sortable(x):
    b = lax.bitcast_convert_type(x, I32)
    return b ^ ((b >> 31) & 0x7FFFFFFF)


def _dsa_kernel(iq_ref, iw_ref, q_ref, ik_ref, k_ref, v_ref, o_ref,
                lhs_ref, qs_ref, key_ref, bias_ref, m_ref, l_ref, acc_ref, thr_ref, jlim_ref,
                *, tq, ch, topk, seq_bits):
    i = pl.program_id(0)
    nch = (i * tq + tq + ch - 1) // ch
    idx_scale = (IDX_HEADS ** -0.5) * (IDX_DIM ** -0.5)
    sm_scale = HEAD_DIM ** -0.5
    nfold = ch // LANES

    for h in range(IDX_HEADS):
        lhs_ref[h * tq:(h + 1) * tq, :] = iq_ref[:, h * IDX_DIM:(h + 1) * IDX_DIM]
    for j in range(N_KV_HEADS_A):
        for g in range(GQA):
            hh = j * GQA + g
            qs_ref[j, g * tq:(g + 1) * tq, :] = q_ref[:, hh * HEAD_DIM:(hh + 1) * HEAD_DIM]

    t_idx = i * tq + lax.broadcasted_iota(I32, (tq, ch), 0)
    lane_idx = lax.broadcasted_iota(I32, (tq, ch), 1)

    def score_body(c, carry):
        d = lax.dot_general(lhs_ref[...], ik_ref[c], (((1,), (1,)), ((), ())), preferred_element_type=F32)
        acc = jnp.zeros((tq, ch), F32)
        for h in range(IDX_HEADS):
            acc = acc + iw_ref[:, h:h + 1] * jnp.maximum(d[h * tq:(h + 1) * tq, :], 0.0)
        sc = acc * idx_scale + 0.0
        s_idx = c * ch + lane_idx
        key_ref[c] = jnp.where(s_idx <= t_idx, _sortable(sc), INT_MIN)
        return carry

    lax.fori_loop(0, nch, score_body, 0)

    def count(pred):
        def body(c, acc):
            m = jnp.where(pred(key_ref[c], c), 1.0, 0.0)
            for f in range(nfold):
                acc = acc + m[:, f * LANES:(f + 1) * LANES]
            return acc
        acc = lax.fori_loop(0, nch, body, jnp.zeros((tq, LANES), F32))
        return jnp.sum(acc, axis=1, keepdims=True)

    def bis_body(it, thr):
        bit = jnp.left_shift(jnp.int32(1), 31 - it)
        cand = thr ^ bit
        cnt = count(lambda k, c: k >= cand)
        return jnp.where(cnt >= topk, cand, thr)

    thr = lax.fori_loop(0, 32, bis_body, jnp.full((tq, 1), INT_MIN, I32))
    thr_ref[...] = thr
    cnt_gt = count(lambda k, c: k > thr)
    cnt_eq = count(lambda k, c: k == thr)
    need = topk - cnt_gt
    jlim_ref[...] = jnp.where(thr == INT_MIN, -1, jnp.int32(2 ** 30))
    excess = jnp.where(thr == INT_MIN, 0.0, cnt_eq - need)

    @pl.when(jnp.max(excess) > 0.0)
    def _():
        def jb(it, a):
            cand = a + jnp.left_shift(jnp.int32(1), seq_bits - 1 - it)
            f = count(lambda k, c: (k == thr) & (c * ch + lane_idx < cand))
            return jnp.where(f < need, cand, a)
        a = lax.fori_loop(0, seq_bits, jb, jnp.zeros((tq, 1), I32))
        jlim_ref[...] = jnp.where(thr == INT_MIN, -1, a)

    def bias_body(c, carry):
        k = key_ref[c]
        t = thr_ref[...]
        sel = (k > t) | ((k == t) & (c * ch + lane_idx <= jlim_ref[...]))
        bias_ref[c] = jnp.where(sel, 0.0, NEG_BIG)
        return carry

    lax.fori_loop(0, nch, bias_body, 0)

    m_ref[...] = jnp.full(m_ref.shape, NEG_BIG, F32)
    l_ref[...] = jnp.zeros(l_ref.shape, F32)
    acc_ref[...] = jnp.zeros(acc_ref.shape, F32)

    def attn_body(c, carry):
        bias = bias_ref[c]
        kc = k_ref[c]
        vc = v_ref[c]
        for j in range(N_KV_HEADS_A):
            hs = slice(j * HEAD_DIM, (j + 1) * HEAD_DIM)
            logits = lax.dot_general(qs_ref[j], kc[:, hs], (((1,), (1,)), ((), ())),
                                     preferred_element_type=F32)
            logits = (logits * sm_scale).reshape(GQA, tq, ch) + bias[None]
            logits = logits.reshape(GQA * tq, ch)
            m_old = m_ref[j]
            m_new = jnp.maximum(m_old, jnp.max(logits, axis=1, keepdims=True))
            alpha = jnp.exp(m_old - m_new)
            p = jnp.exp(logits - m_new)
            l_ref[j] = alpha * l_ref[j] + jnp.sum(p, axis=1, keepdims=True)
            acc_ref[j] = alpha * acc_ref[j] + jnp.dot(p.astype(BF16), vc[:, hs], preferred_element_type=F32)
            m_ref[j] = m_new
        return carry

    lax.fori_loop(0, nch, attn_body, 0)

    for j in range(N_KV_HEADS_A):
        o = acc_ref[j] / l_ref[j]
        for g in range(GQA):
            hh = j * GQA + g
            o_ref[:, hh * HEAD_DIM:(hh + 1) * HEAD_DIM] = o[g * tq:(g + 1) * tq, :].astype(o_ref.dtype)


def dsa_attention(qkvi, ik, iw, topk):
    S = qkvi.shape[0]
    tq = 128
    ch = _pick(S, (512, 256, 128))
    nc = S // ch
    seq_bits = max(1, (S - 1).bit_length())
    k3 = qkvi.reshape(nc, ch, W_QKVI)
    ik3 = ik.reshape(nc, ch, IDX_DIM)
    kb = ATTN_WIDTH // KV_WIDTH
    ib = (ATTN_WIDTH + 2 * KV_WIDTH) // IDX_WIDTH
    return pl.pallas_call(
        functools.partial(_dsa_kernel, tq=tq, ch=ch, topk=topk, seq_bits=seq_bits),
        grid=(S // tq,),
        in_specs=[
            pl.BlockSpec((tq, IDX_WIDTH), lambda i: (i, ib)),
            pl.BlockSpec((tq, IDX_HEADS), lambda i: (i, 0)),
            pl.BlockSpec((tq, ATTN_WIDTH), lambda i: (i, 0)),
            _resident((nc, ch, IDX_DIM), lambda i: (0, 0, 0)),
            _resident((nc, ch, KV_WIDTH), lambda i: (0, 0, kb)),
            _resident((nc, ch, KV_WIDTH), lambda i: (0, 0, kb + 1)),
        ],
        out_specs=pl.BlockSpec((tq, ATTN_WIDTH), lambda i: (i, 0)),
        out_shape=jax.ShapeDtypeStruct((S, ATTN_WIDTH), BF16),
        scratch_shapes=[
            pltpu.VMEM((IDX_HEADS * tq, IDX_DIM), BF16),
            pltpu.VMEM((N_KV_HEADS_A, GQA * tq, HEAD_DIM), BF16),
            pltpu.VMEM((nc, tq, ch), I32),
            pltpu.VMEM((nc, tq, ch), F32),
            pltpu.VMEM((N_KV_HEADS_A, GQA * tq, 1), F32),
            pltpu.VMEM((N_KV_HEADS_A, GQA * tq, 1), F32),
            pltpu.VMEM((N_KV_HEADS_A, GQA * tq, HEAD_DIM), F32),
            pltpu.VMEM((tq, 1), I32),
            pltpu.VMEM((tq, 1), I32),
        ],
        compiler_params=_cparams(("parallel",)),
        name="dsa_attention",
    )(qkvi, iw, qkvi, ik3, k3, k3)


def kernel(x, positions, ffn1_norm, ffn1_w_in, ffn1_w_out, mix_norm, w_in, sgu_ln_g, sgu_ln_b, sgu_w, sgu_b,
           pool_w, pool_scale, proj_a, proj_b, proj_c, w_out, ffn2_norm, ffn2_w_in, ffn2_w_out, final_norm):
    B, S, D = x.shape
    assert B == 1, "kernel is written for a single sequence"
    depth = w_in.shape[0]
    topk = min(TOPK_MAX, S // 4)
    pos = positions[0]
    tabs128 = _rope_tables(pos, HEAD_DIM, LANES, LANES)
    tabs64 = _rope_tables(pos, IDX_DIM, IDX_DIM, LANES)
    tabs_ik = _rope_tables(pos, IDX_DIM, IDX_DIM, IDX_DIM)

    h = x[0]
    for l in range(depth):
        n = rmsnorm(h, ffn1_norm[l], BF16)
        a = ffn_up(n, ffn1_w_in[l].astype(BF16))
        h = matmul_residual(a, ffn1_w_out[l].astype(BF16), h, 0.5)

        n = rmsnorm(h, mix_norm[l], BF16)
        wl = w_in[l]
        w_qkvi = wl[:, OFF_QKVI:OFF_QKVI + W_QKVI].astype(BF16)
        w_ikiw = jnp.pad(wl[:, OFF_IKIW:OFF_IKIW + W_IKIW], ((0, 0), (0, LANES - W_IKIW))).astype(BF16)
        w_z = wl[:, OFF_Z:OFF_Z + 2 * SGU_WIDTH].astype(BF16)
        w_c = wl[:, OFF_C:OFF_C + POOL_WIDTH].astype(BF16)
        w_g = wl[:, OFF_G:OFF_G + N_BRANCHES * D].astype(BF16)

        qkvi = qkvi_proj(n, w_qkvi, tabs128, tabs64)
        ikiw = ikiw_proj(n, w_ikiw, tabs_ik)
        ik = ikiw[:, :IDX_DIM].astype(BF16)
        iw = ikiw[:, IDX_DIM:IDX_DIM + IDX_HEADS]
        y_a = dsa_attention(qkvi, ik, iw, topk)
        y_b = sgu_branch(n, w_z, sgu_ln_g[l], sgu_ln_b[l], sgu_w[l], sgu_b[l])
        y_c = pool_branch(n, w_c, pool_w[l].astype(BF16), pool_scale[l])
        merged = gated_merge(n, y_a, y_b, y_c, w_g, proj_a[l].astype(BF16), proj_b[l].astype(BF16),
                             proj_c[l].astype(BF16))
        h = matmul_residual(merged, w_out[l].astype(BF16), h, 1.0)

        n = rmsnorm(h, ffn2_norm[l], BF16)
        a = ffn_up(n, ffn2_w_in[l].astype(BF16))
        h = matmul_residual(a, ffn2_w_out[l].astype(BF16), h, 0.5)

    return rmsnorm(h, final_norm, F32)[None]
```

```python
import functools
import math

import jax
import jax.numpy as jnp
from jax import lax
from jax.experimental import pallas as pl
from jax.experimental.pallas import tpu as pltpu

F32 = jnp.float32
BF16 = jnp.bfloat16
I32 = jnp.int32

HEAD_DIM = 128
N_HEADS_A = 16
N_KV_HEADS_A = 4
GQA = N_HEADS_A // N_KV_HEADS_A
ATTN_WIDTH = N_HEADS_A * HEAD_DIM
KV_WIDTH = N_KV_HEADS_A * HEAD_DIM
IDX_HEADS = 16
IDX_DIM = 64
IDX_WIDTH = IDX_HEADS * IDX_DIM
TOPK_MAX = 256
ROPE_THETA = 500000.0
ROPE_FRAC_DEN = 4
SGU_CHUNK = 128
SGU_GROUPS = 8
SGU_GROUP_DIM = 128
SGU_WIDTH = SGU_GROUPS * SGU_GROUP_DIM
POOL_WINDOWS = (2, 4, 8, 16)
POOL_GROUPS = 4
POOL_GROUP_DIM = 256
POOL_WIDTH = POOL_GROUPS * POOL_GROUP_DIM
POOL_HALO = 16
N_BRANCHES = 3
EPS = 1e-6

LANES = 128
VMEM_LIMIT = 56 * 1024 * 1024
COUNT_ROWS = 64
INT_MIN = -(2 ** 31)
NEG_BIG = -1e30

OFF_QKVI = 0
W_QKVI = ATTN_WIDTH + 2 * KV_WIDTH + IDX_WIDTH
OFF_IKIW = W_QKVI
W_IKIW = IDX_DIM + IDX_HEADS
OFF_Z = OFF_IKIW + W_IKIW
OFF_C = OFF_Z + 2 * SGU_WIDTH
OFF_G = OFF_C + POOL_WIDTH


def _pick(n, prefs):
    for p in prefs:
        if n % p == 0:
            return p
    return n


def _cparams(sem):
    return pltpu.CompilerParams(dimension_semantics=sem, vmem_limit_bytes=VMEM_LIMIT)


def _resident(shape, index_map):
    return pl.BlockSpec(shape, index_map, pipeline_mode=pl.Buffered(1))


def _rmsnorm_kernel(h_ref, g_ref, o_ref):
    x = h_ref[...]
    ms = jnp.mean(x * x, axis=-1, keepdims=True)
    y = x * lax.rsqrt(ms + EPS)
    o_ref[...] = (y * g_ref[...]).astype(o_ref.dtype)


def rmsnorm(h, gain, out_dtype):
    S, D = h.shape
    tm = _pick(S, (256, 128, 8))
    return pl.pallas_call(
        _rmsnorm_kernel,
        grid=(S // tm,),
        in_specs=[pl.BlockSpec((tm, D), lambda i: (i, 0)), pl.BlockSpec((1, D), lambda i: (0, 0))],
        out_specs=pl.BlockSpec((tm, D), lambda i: (i, 0)),
        out_shape=jax.ShapeDtypeStruct((S, D), out_dtype),
        compiler_params=_cparams(("parallel",)),
        name="rmsnorm",
    )(h, gain.reshape(1, D))


def _ffn_up_kernel(n_ref, wg_ref, wu_ref, o_ref):
    n = n_ref[...]
    g = jnp.dot(n, wg_ref[...].astype(BF16), preferred_element_type=F32)
    u = jnp.dot(n, wu_ref[...].astype(BF16), preferred_element_type=F32)
    o_ref[...] = (g * jax.nn.sigmoid(g) * u).astype(o_ref.dtype)


def ffn_up(n, w_in, layer):
    S, D = n.shape
    F = w_in.shape[2] // 2
    tm = _pick(S, (1024, 512, 256, 128))
    tn = _pick(F, (256, 128))
    nj = F // tn
    return pl.pallas_call(
        _ffn_up_kernel,
        grid=(S // tm, nj),
        in_specs=[
            pl.BlockSpec((tm, D), lambda i, j: (i, 0)),
            pl.BlockSpec((None, D, tn), lambda i, j: (layer, 0, j)),
            pl.BlockSpec((None, D, tn), lambda i, j: (layer, 0, j + nj)),
        ],
        out_specs=pl.BlockSpec((tm, tn), lambda i, j: (i, j)),
        out_shape=jax.ShapeDtypeStruct((S, F), BF16),
        compiler_params=_cparams(("parallel", "arbitrary")),
        name="ffn_up",
    )(n, w_in, w_in)


def _mm_res_kernel(a_ref, w_ref, h_ref, o_ref, *, scale):
    acc = jnp.dot(a_ref[...], w_ref[...], preferred_element_type=F32)
    o_ref[...] = h_ref[...] + scale * acc


def matmul_residual(a, w, h, scale):
    S, K = a.shape
    N = w.shape[1]
    tm = _pick(S, (512, 256, 128))
    tn = _pick(N, (512, 256, 128))
    return pl.pallas_call(
        functools.partial(_mm_res_kernel, scale=scale),
        grid=(S // tm, N // tn),
        in_specs=[
            pl.BlockSpec((tm, K), lambda i, j: (i, 0)),
            pl.BlockSpec((K, tn), lambda i, j: (0, j)),
            pl.BlockSpec((tm, tn), lambda i, j: (i, j)),
        ],
        out_specs=pl.BlockSpec((tm, tn), lambda i, j: (i, j)),
        out_shape=jax.ShapeDtypeStruct((S, N), F32),
        compiler_params=_cparams(("parallel", "arbitrary")),
        name="matmul_residual",
    )(a, w, h)


def _rope_tables(positions, head_dim, period, active_lanes):
    rd = head_dim // ROPE_FRAC_DEN
    half = rd // 2
    inv_freq = ROPE_THETA ** (-(jnp.arange(0, rd, 2, dtype=F32) / rd))
    ang = positions.astype(F32)[:, None] * inv_freq[None, :]
    cos, sin = jnp.cos(ang), jnp.sin(ang)
    lane = jnp.arange(LANES)
    p = lane % period
    act = lane < active_lanes
    idx = p % half
    cos_l, sin_l = cos[:, idx], sin[:, idx]
    first = act & (p < half)
    second = act & (p >= half) & (p < rd)
    c = jnp.where(first | second, cos_l, 1.0)
    s1 = jnp.where(first, -sin_l, 0.0)
    s2 = jnp.where(second, sin_l, 0.0)
    return c.astype(F32), s1.astype(F32), s2.astype(F32), half


def _rope_apply(x, c, s1, s2, half):
    outs = []
    for b in range(x.shape[1] // LANES):
        xb = x[:, b * LANES:(b + 1) * LANES]
        outs.append(xb * c + pltpu.roll(xb, LANES - half, 1) * s1 + pltpu.roll(xb, half, 1) * s2)
    return outs[0] if len(outs) == 1 else jnp.concatenate(outs, axis=1)


def _qkvi_kernel(n_ref, w_ref, c128_ref, s1_128_ref, s2_128_ref, c64_ref, s1_64_ref, s2_64_ref, o_ref,
                 *, tn, half128, half64):
    j = pl.program_id(1)
    acc = jnp.dot(n_ref[...], w_ref[...], preferred_element_type=F32)
    n_qk = (ATTN_WIDTH + KV_WIDTH) // tn
    n_v = KV_WIDTH // tn

    @pl.when(j < n_qk)
    def _():
        o_ref[...] = _rope_apply(acc, c128_ref[...], s1_128_ref[...], s2_128_ref[...], half128).astype(o_ref.dtype)

    @pl.when((j >= n_qk) & (j < n_qk + n_v))
    def _():
        o_ref[...] = acc.astype(o_ref.dtype)

    @pl.when(j >= n_qk + n_v)
    def _():
        o_ref[...] = _rope_apply(acc, c64_ref[...], s1_64_ref[...], s2_64_ref[...], half64).astype(o_ref.dtype)


def qkvi_proj(n, w, tabs128, tabs64):
    S, D = n.shape
    N = w.shape[1]
    tm = _pick(S, (1024, 512, 256, 128))
    tn = 512
    c128, s1_128, s2_128, half128 = tabs128
    c64, s1_64, s2_64, half64 = tabs64
    tab = pl.BlockSpec((tm, LANES), lambda i, j: (i, 0))
    return pl.pallas_call(
        functools.partial(_qkvi_kernel, tn=tn, half128=half128, half64=half64),
        grid=(S // tm, N // tn),
        in_specs=[pl.BlockSpec((tm, D), lambda i, j: (i, 0)), pl.BlockSpec((D, tn), lambda i, j: (0, j)),
                  tab, tab, tab, tab, tab, tab],
        out_specs=pl.BlockSpec((tm, tn), lambda i, j: (i, j)),
        out_shape=jax.ShapeDtypeStruct((S, N), BF16),
        compiler_params=_cparams(("parallel", "arbitrary")),
        name="qkvi_proj",
    )(n, w, c128, s1_128, s2_128, c64, s1_64, s2_64)


def _ikiw_kernel(n_ref, w_ref, c_ref, s1_ref, s2_ref, o_ref, *, half):
    acc = jnp.dot(n_ref[...], w_ref[...], preferred_element_type=F32)
    o_ref[...] = _rope_apply(acc, c_ref[...], s1_ref[...], s2_ref[...], half)


def ikiw_proj(n, w, tabs):
    S, D = n.shape
    tm = _pick(S, (1024, 512, 256, 128))
    c, s1, s2, half = tabs
    tab = pl.BlockSpec((tm, LANES), lambda i: (i, 0))
    return pl.pallas_call(
        functools.partial(_ikiw_kernel, half=half),
        grid=(S // tm,),
        in_specs=[pl.BlockSpec((tm, D), lambda i: (i, 0)), _resident((D, LANES), lambda i: (0, 0)), tab, tab, tab],
        out_specs=pl.BlockSpec((tm, LANES), lambda i: (i, 0)),
        out_shape=jax.ShapeDtypeStruct((S, LANES), F32),
        compiler_params=_cparams(("parallel",)),
        name="ikiw_proj",
    )(n, w, c, s1, s2)


def _sgu_kernel(n_ref, wz_ref, lng_ref, lnb_ref, ws_ref, bst_ref, o_ref, *, tm):
    z = jnp.dot(n_ref[...], wz_ref[...], preferred_element_type=F32)
    z = jax.nn.gelu(z)
    u = z[:, :SGU_WIDTH]
    v = z[:, SGU_WIDTH:]
    mu = jnp.mean(v, axis=-1, keepdims=True)
    var = jnp.mean(jnp.square(v - mu), axis=-1, keepdims=True)
    vn = ((v - mu) * lax.rsqrt(var + EPS) * lng_ref[...] + lnb_ref[...]).astype(BF16)
    row = lax.broadcasted_iota(I32, (SGU_CHUNK, SGU_CHUNK), 0)
    col = lax.broadcasted_iota(I32, (SGU_CHUNK, SGU_CHUNK), 1)
    causal = col <= row
    for g in range(SGU_GROUPS):
        w = jnp.where(causal, ws_ref[g], 0.0).astype(BF16)
        bias = bst_ref[:, g:g + 1]
        gs = slice(g * SGU_GROUP_DIM, (g + 1) * SGU_GROUP_DIM)
        for c in range(tm // SGU_CHUNK):
            rs = slice(c * SGU_CHUNK, (c + 1) * SGU_CHUNK)
            s = jnp.dot(w, vn[rs, gs], preferred_element_type=F32) + bias
            o_ref[rs, gs] = (u[rs, gs] * s).astype(o_ref.dtype)


def sgu_branch(n, wz, ln_g, ln_b, w_s, b_s):
    S, D = n.shape
    tm = _pick(S, (512, 256, 128))
    return pl.pallas_call(
        functools.partial(_sgu_kernel, tm=tm),
        grid=(S // tm,),
        in_specs=[
            pl.BlockSpec((tm, D), lambda i: (i, 0)),
            _resident((D, 2 * SGU_WIDTH), lambda i: (0, 0)),
            pl.BlockSpec((1, SGU_WIDTH), lambda i: (0, 0)),
            pl.BlockSpec((1, SGU_WIDTH), lambda i: (0, 0)),
            pl.BlockSpec((SGU_GROUPS, SGU_CHUNK, SGU_CHUNK), lambda i: (0, 0, 0)),
            pl.BlockSpec((SGU_CHUNK, SGU_GROUPS), lambda i: (0, 0)),
        ],
        out_specs=pl.BlockSpec((tm, SGU_WIDTH), lambda i: (i, 0)),
        out_shape=jax.ShapeDtypeStruct((S, SGU_WIDTH), BF16),
        compiler_params=_cparams(("parallel",)),
        name="sgu_branch",
    )(n, wz, ln_g.reshape(1, SGU_WIDTH), ln_b.reshape(1, SGU_WIDTH), w_s, b_s.T)


def _pool_kernel(n_ref, wc_ref, pw_ref, sc_ref, o_ref, ext_ref, *, tm):
    i = pl.program_id(0)

    @pl.when(i == 0)
    def _():
        ext_ref[0:POOL_HALO, :] = jnp.zeros((POOL_HALO, POOL_WIDTH), F32)

    c = jnp.dot(n_ref[...], wc_ref[...], preferred_element_type=F32)
    ext_ref[POOL_HALO:POOL_HALO + tm, :] = c
    t = i * tm + lax.broadcasted_iota(I32, (tm, 1), 0)
    for g, win in enumerate(POOL_WINDOWS):
        gs = slice(g * POOL_GROUP_DIM, (g + 1) * POOL_GROUP_DIM)
        cg = c[:, gs]
        s = cg
        for d in range(1, win):
            s = s + ext_ref[POOL_HALO - d:POOL_HALO - d + tm, gs]
        count = jnp.minimum(t + 1, win).astype(F32)
        y = (s / count - cg).astype(BF16)
        out = jnp.dot(y, pw_ref[g], preferred_element_type=F32) * sc_ref[:, gs]
        o_ref[:, gs] = out.astype(o_ref.dtype)
    ext_ref[0:POOL_HALO, :] = c[tm - POOL_HALO:, :]


def pool_branch(n, wc, pool_w, pool_scale):
    S, D = n.shape
    tm = _pick(S, (512, 256, 128))
    return pl.pallas_call(
        functools.partial(_pool_kernel, tm=tm),
        grid=(S // tm,),
        in_specs=[
            pl.BlockSpec((tm, D), lambda i: (i, 0)),
            _resident((D, POOL_WIDTH), lambda i: (0, 0)),
            pl.BlockSpec((POOL_GROUPS, POOL_GROUP_DIM, POOL_GROUP_DIM), lambda i: (0, 0, 0)),
            pl.BlockSpec((1, POOL_WIDTH), lambda i: (0, 0)),
        ],
        out_specs=pl.BlockSpec((tm, POOL_WIDTH), lambda i: (i, 0)),
        out_shape=jax.ShapeDtypeStruct((S, POOL_WIDTH), BF16),
        scratch_shapes=[pltpu.VMEM((POOL_HALO + tm, POOL_WIDTH), F32)],
        compiler_params=_cparams(("arbitrary",)),
        name="pool_branch",
    )(n, wc, pool_w, pool_scale.reshape(1, POOL_WIDTH))


def _merge_kernel(n_ref, ya_ref, yb_ref, yc_ref, wga_ref, wgb_ref, wgc_ref, pa_ref, pb_ref, pc_ref, o_ref):
    n = n_ref[...]

    def branch(wg_ref, y_ref, p_ref):
        gate = jax.nn.sigmoid(jnp.dot(n, wg_ref[...], preferred_element_type=F32))
        return gate * jnp.dot(y_ref[...], p_ref[...], preferred_element_type=F32)

    m = branch(wga_ref, ya_ref, pa_ref) + branch(wgb_ref, yb_ref, pb_ref) + branch(wgc_ref, yc_ref, pc_ref)
    o_ref[...] = m.astype(o_ref.dtype)


def gated_merge(n, ya, yb, yc, wgates, pa, pb, pc):
    S, D = n.shape
    tm = _pick(S, (512, 256, 128))
    tn = _pick(D, (256, 128))
    nj = D // tn
    row = lambda w: pl.BlockSpec((tm, w), lambda i, j: (i, 0))
    col = lambda k, off: pl.BlockSpec((k, tn), lambda i, j: (0, j + off))
    return pl.pallas_call(
        _merge_kernel,
        grid=(S // tm, nj),
        in_specs=[row(D), row(ATTN_WIDTH), row(SGU_WIDTH), row(POOL_WIDTH),
                  col(D, 0), col(D, nj), col(D, 2 * nj),
                  col(ATTN_WIDTH, 0), col(SGU_WIDTH, 0), col(POOL_WIDTH, 0)],
        out_specs=pl.BlockSpec((tm, tn), lambda i, j: (i, j)),
        out_shape=jax.ShapeDtypeStruct((S, D), BF16),
        compiler_params=_cparams(("parallel", "arbitrary")),
        name="gated_merge",
    )(n, ya, yb, yc, wgates, wgates, wgates, pa, pb, pc)


def _sortable(x):
    b = lax.bitcast_convert_type(x, I32)
    return b ^ ((b >> 31) & 0x7FFFFFFF)


def _dsa_kernel(iq_ref, iwt_ref, q_ref, ik_ref, k_ref, v_ref, o_ref,
                lhs_ref, qe_ref, key_ref, bias_ref, m_ref, acc_ref, thr_ref, jlim_ref,
                *, tq, ch, topk, seq_bits):
    i = pl.program_id(0)
    nch = (i * tq + tq + ch - 1) // ch
    idx_scale = (IDX_HEADS ** -0.5) * (IDX_DIM ** -0.5)
    c_exp = (HEAD_DIM ** -0.5) * math.log2(math.e)
    nt = (((1,), (1,)), ((), ()))

    for h in range(IDX_HEADS):
        lhs_ref[h * tq:(h + 1) * tq, :] = iq_ref[:, h * IDX_DIM:(h + 1) * IDX_DIM]
    eye = (lax.broadcasted_iota(I32, (tq, LANES), 0) == lax.broadcasted_iota(I32, (tq, LANES), 1)).astype(BF16)
    for j in range(N_KV_HEADS_A):
        for g in range(GQA):
            hh = j * GQA + g
            qe_ref[j, g * tq:(g + 1) * tq, :HEAD_DIM] = q_ref[:, hh * HEAD_DIM:(hh + 1) * HEAD_DIM]
            qe_ref[j, g * tq:(g + 1) * tq, HEAD_DIM:] = eye

    s_loc = lax.broadcasted_iota(I32, (ch, LANES), 0)
    t_idx = i * tq + lax.broadcasted_iota(I32, (ch, LANES), 1)

    def score_body(c, carry):
        d = lax.dot_general(ik_ref[c], lhs_ref[...], nt, preferred_element_type=F32)
        acc = jnp.zeros((ch, LANES), F32)
        for h in range(IDX_HEADS):
            acc = acc + iwt_ref[h:h + 1, :] * jnp.maximum(d[:, h * tq:(h + 1) * tq], 0.0)
        sc = acc * idx_scale + 0.0
        key_ref[c] = jnp.where(c * ch + s_loc <= t_idx, _sortable(sc), INT_MIN)
        return carry

    lax.fori_loop(0, nch, score_body, 0)

    def count(pred):
        def body(c, acc):
            m = jnp.where(pred(key_ref[c], c), 1.0, 0.0)
            return acc + jnp.sum(m.reshape(ch // COUNT_ROWS, COUNT_ROWS, LANES), axis=0)
        acc = lax.fori_loop(0, nch, body, jnp.zeros((COUNT_ROWS, LANES), F32))
        return jnp.sum(acc, axis=0, keepdims=True)

    def bis_body(it, thr):
        bit = jnp.left_shift(jnp.int32(1), 31 - it)
        cand = thr ^ bit
        cnt = count(lambda k, c: k >= cand)
        return jnp.where(cnt >= topk, cand, thr)

    thr = lax.fori_loop(0, 32, bis_body, jnp.full((1, LANES), INT_MIN, I32))
    thr_ref[...] = thr
    cnt_gt = count(lambda k, c: k > thr)
    cnt_eq = count(lambda k, c: k == thr)
    need = topk - cnt_gt
    jlim_ref[...] = jnp.where(thr == INT_MIN, -1, jnp.int32(2 ** 30))
    excess = jnp.where(thr == INT_MIN, 0.0, cnt_eq - need)

    @pl.when(jnp.max(excess) > 0.0)
    def _():
        def jb(it, a):
            cand = a + jnp.left_shift(jnp.int32(1), seq_bits - 1 - it)
            f = count(lambda k, c: (k == thr) & (c * ch + s_loc < cand))
            return jnp.where(f < need, cand, a)
        a = lax.fori_loop(0, seq_bits, jb, jnp.zeros((1, LANES), I32))
        jlim_ref[...] = jnp.where(thr == INT_MIN, -1, a)

    def bias_body(c, carry):
        k = key_ref[c]
        t = thr_ref[...]
        sel = (k > t) | ((k == t) & (c * ch + s_loc <= jlim_ref[...]))
        bias_ref[c] = jnp.where(sel, 0.0, NEG_BIG).astype(BF16)
        return carry

    lax.fori_loop(0, nch, bias_body, 0)

    m_ref[...] = jnp.full(m_ref.shape, NEG_BIG, F32)
    acc_ref[...] = jnp.zeros(acc_ref.shape, F32)
    ones = jnp.ones((ch, HEAD_DIM), BF16)

    def attn_body(c, carry):
        kc = k_ref[c]
        vc = v_ref[c]
        bias_t = bias_ref[c]
        for j in range(N_KV_HEADS_A):
            hs = slice(j * HEAD_DIM, (j + 1) * HEAD_DIM)
            rhs = jnp.concatenate([kc[:, hs], bias_t], axis=1)
            logits = lax.dot_general(qe_ref[j], rhs, nt, preferred_element_type=F32)
            m_prev = m_ref[j]
            m_next = jnp.maximum(m_prev, jnp.max(logits, axis=1, keepdims=True))
            alpha = jnp.exp2((m_prev - m_next) * c_exp)
            p = jnp.concatenate(
                [jnp.exp2((logits[:, f * LANES:(f + 1) * LANES] - m_next) * c_exp) for f in range(ch // LANES)],
                axis=1).astype(BF16)
            vaug = jnp.concatenate([vc[:, hs], ones], axis=1)
            pv = jnp.dot(p, vaug, preferred_element_type=F32)
            acc_ref[j] = jnp.concatenate([alpha, alpha], axis=1) * acc_ref[j] + pv
            m_ref[j] = m_next
        return carry

    lax.fori_loop(0, nch, attn_body, 0)

    for j in range(N_KV_HEADS_A):
        acc = acc_ref[j]
        o = acc[:, :HEAD_DIM] / acc[:, HEAD_DIM:]
        for g in range(GQA):
            hh = j * GQA + g
            o_ref[:, hh * HEAD_DIM:(hh + 1) * HEAD_DIM] = o[g * tq:(g + 1) * tq, :].astype(o_ref.dtype)


def dsa_attention(qkvi, ik, iw, topk):
    S = qkvi.shape[0]
    tq = LANES
    ch = _pick(S, (512, 256, 128))
    nc = S // ch
    seq_bits = max(1, (S - 1).bit_length())
    k3 = qkvi.reshape(nc, ch, W_QKVI)
    ik3 = ik.reshape(nc, ch, IDX_DIM)
    kb = ATTN_WIDTH // KV_WIDTH
    ib = (ATTN_WIDTH + 2 * KV_WIDTH) // IDX_WIDTH
    return pl.pallas_call(
        functools.partial(_dsa_kernel, tq=tq, ch=ch, topk=topk, seq_bits=seq_bits),
        grid=(S // tq,),
        in_specs=[
            pl.BlockSpec((tq, IDX_WIDTH), lambda i: (i, ib)),
            pl.BlockSpec((IDX_HEADS, tq), lambda i: (0, i)),
            pl.BlockSpec((tq, ATTN_WIDTH), lambda i: (i, 0)),
            _resident((nc, ch, IDX_DIM), lambda i: (0, 0, 0)),
            _resident((nc, ch, KV_WIDTH), lambda i: (0, 0, kb)),
            _resident((nc, ch, KV_WIDTH), lambda i: (0, 0, kb + 1)),
        ],
        out_specs=pl.BlockSpec((tq, ATTN_WIDTH), lambda i: (i, 0)),
        out_shape=jax.ShapeDtypeStruct((S, ATTN_WIDTH), BF16),
        scratch_shapes=[
            pltpu.VMEM((IDX_HEADS * tq, IDX_DIM), BF16),
            pltpu.VMEM((N_KV_HEADS_A, GQA * tq, HEAD_DIM + tq), BF16),
            pltpu.VMEM((nc, ch, tq), I32),
            pltpu.VMEM((nc, ch, tq), BF16),
            pltpu.VMEM((N_KV_HEADS_A, GQA * tq, LANES), F32),
            pltpu.VMEM((N_KV_HEADS_A, GQA * tq, 2 * HEAD_DIM), F32),
            pltpu.VMEM((1, tq), I32),
            pltpu.VMEM((1, tq), I32),
        ],
        compiler_params=_cparams(("parallel",)),
        name="dsa_attention",
    )(qkvi, iw.T, qkvi, ik3, k3, k3)


def kernel(x, positions, ffn1_norm, ffn1_w_in, ffn1_w_out, mix_norm, w_in, sgu_ln_g, sgu_ln_b, sgu_w, sgu_b,
           pool_w, pool_scale, proj_a, proj_b, proj_c, w_out, ffn2_norm, ffn2_w_in, ffn2_w_out, final_norm):
    B, S, D = x.shape
    assert B == 1, "kernel is written for a single sequence"
    depth = w_in.shape[0]
    topk = min(TOPK_MAX, S // 4)
    pos = positions[0]
    tabs128 = _rope_tables(pos, HEAD_DIM, LANES, LANES)
    tabs64 = _rope_tables(pos, IDX_DIM, IDX_DIM, LANES)
    tabs_ik = _rope_tables(pos, IDX_DIM, IDX_DIM, IDX_DIM)

    h = x[0]
    for l in range(depth):
        n = rmsnorm(h, ffn1_norm[l], BF16)
        a = ffn_up(n, ffn1_w_in, l)
        h = matmul_residual(a, ffn1_w_out[l].astype(BF16), h, 0.5)

        n = rmsnorm(h, mix_norm[l], BF16)
        wl = w_in[l]
        w_qkvi = wl[:, OFF_QKVI:OFF_QKVI + W_QKVI].astype(BF16)
        w_ikiw = jnp.pad(wl[:, OFF_IKIW:OFF_IKIW + W_IKIW], ((0, 0), (0, LANES - W_IKIW))).astype(BF16)
        w_z = wl[:, OFF_Z:OFF_Z + 2 * SGU_WIDTH].astype(BF16)
        w_c = wl[:, OFF_C:OFF_C + POOL_WIDTH].astype(BF16)
        w_g = wl[:, OFF_G:OFF_G + N_BRANCHES * D].astype(BF16)

        qkvi = qkvi_proj(n, w_qkvi, tabs128, tabs64)
        ikiw = ikiw_proj(n, w_ikiw, tabs_ik)
        ik = ikiw[:, :IDX_DIM].astype(BF16)
        iw = ikiw[:, IDX_DIM:IDX_DIM + IDX_HEADS]
        y_a = dsa_attention(qkvi, ik, iw, topk)
        y_b = sgu_branch(n, w_z, sgu_ln_g[l], sgu_ln_b[l], sgu_w[l], sgu_b[l])
        y_c = pool_branch(n, w_c, pool_w[l].astype(BF16), pool_scale[l])
        merged = gated_merge(n, y_a, y_b, y_c, w_g, proj_a[l].astype(BF16), proj_b[l].astype(BF16),
                             proj_c[l].astype(BF16))
        h = matmul_residual(merged, w_out[l].astype(BF16), h, 1.0)

        n = rmsnorm(h, ffn2_norm[l], BF16)
        a = ffn_up(n, ffn2_w_in, l)
        h = matmul_residual(a, ffn2_w_out[l].astype(BF16), h, 0.5)

    return rmsnorm(h, final_norm, F32)[None]
```
